```python
import math
import jax, jax.numpy as jnp
from jax import lax
import numpy as np

D_MODEL = 1024
BATCH = 2
SEQ = 16384
DEPTH = 2

GRID_W = 64
CTX_LEN = 256
NORM_EPS = 1e-6

HEAD_DIM = 64
DIFF_HEADS = 4
DIFF_VDIM = 2 * HEAD_DIM
DIFF_WIDTH = DIFF_HEADS * DIFF_VDIM
DIFF_PROJ = 3 * DIFF_WIDTH
DIFF_SUBLN_EPS = 1e-5
Q_BLOCK = 128
ROPE_BASE = 10000.0
ROPE_AXIS_DIM = HEAD_DIM // 2

RWKV_HEADS = 8
RWKV_WIDTH = RWKV_HEADS * HEAD_DIM
DECAY_LORA = 64
ICLR_LORA = 64
GATE_LORA = 128
RWKV_PROJ = 3 * RWKV_WIDTH + DECAY_LORA + ICLR_LORA + GATE_LORA
RWKV_SPLITS = (RWKV_WIDTH, 2 * RWKV_WIDTH, 3 * RWKV_WIDTH, 3 * RWKV_WIDTH + DECAY_LORA, 3 * RWKV_WIDTH + DECAY_LORA + ICLR_LORA)
RWKV_GN_EPS = 64e-5

EVEN_PROJ = DIFF_PROJ + RWKV_PROJ
MIX_WIDTH = DIFF_WIDTH + RWKV_WIDTH

SSM_INNER = 2 * D_MODEL
SSM_HEAD_DIM = 64
SSM_HEADS = SSM_INNER // SSM_HEAD_DIM
SSM_GROUPS = 4
SSM_HEADS_PER_GROUP = SSM_HEADS // SSM_GROUPS
SSM_STATE = 128
SSM_CONV = 5
SSM_CHUNK = 128
SSM_CONV_CH = SSM_INNER + 2 * SSM_GROUPS * SSM_STATE
SSM_PROJ = SSM_INNER + SSM_CONV_CH + 2 * SSM_HEADS
SSM_NORM_EPS = 1e-5

FFN_HIDDEN = 2816
FFN_CONV = 3

N_EVEN = (DEPTH + 1) // 2
N_ODD = DEPTH // 2

kernel_name = 'hybrid_diffattn_rwkv7_mamba2_dit'


def rmsnorm(x, w, eps=NORM_EPS):
    xf = x.astype(jnp.float32)
    y = xf * lax.rsqrt(jnp.mean(xf * xf, axis=-1, keepdims=True) + eps)
    return (y * w.astype(jnp.float32)).astype(x.dtype)


def modulate(h, shift, scale):
    return h * (1 + scale) + shift


def dwconv_centred(x, w, b):
    k = w.shape[0]
    pad = k // 2
    y = lax.conv_general_dilated(x, w[:, None, :].astype(x.dtype), window_strides=(1,), padding=[(pad, pad)],
                                 dimension_numbers=('NWC', 'WIO', 'NWC'), feature_group_count=x.shape[-1])
    return y + b


def token_shift_centred(p, mu_prev, mu_next):
    prev = jnp.pad(p[:, :-1], ((0, 0), (1, 0), (0, 0)))
    nxt = jnp.pad(p[:, 1:], ((0, 0), (0, 1), (0, 0)))
    return p + mu_prev * (prev - p) + mu_next * (nxt - p)


def axial_rope_angles(rows):
    row = jnp.repeat(jnp.arange(rows, dtype=jnp.float32), GRID_W)
    col = (jnp.arange(rows * GRID_W) % GRID_W).astype(jnp.float32)
    inv = ROPE_BASE ** (-jnp.arange(0, ROPE_AXIS_DIM, 2, dtype=jnp.float32) / ROPE_AXIS_DIM)
    return (row[:, None] * inv)[None, :, None, None, :], (col[:, None] * inv)[None, :, None, None, :]


def rope_1d(x, ang):
    cos = jnp.cos(ang).astype(x.dtype)
    sin = jnp.sin(ang).astype(x.dtype)
    x1, x2 = jnp.split(x, 2, axis=-1)
    return jnp.concatenate([x1 * cos - x2 * sin, x2 * cos + x1 * sin], axis=-1)


def rope_axial(x, ang_row, ang_col):
    xr, xc = jnp.split(x, 2, axis=-1)
    return jnp.concatenate([rope_1d(xr, ang_row), rope_1d(xc, ang_col)], axis=-1)


def split_diff(d):
    b, t, _ = d.shape
    q, k, v = jnp.split(d, 3, axis=-1)
    return (q.reshape(b, t, DIFF_HEADS, 2, HEAD_DIM), k.reshape(b, t, DIFF_HEADS, 2, HEAD_DIM),
            v.reshape(b, t, DIFF_HEADS, DIFF_VDIM))


def diff_attention(q, k_all, v_all, lam):
    b, t = q.shape[:2]
    nb = t // Q_BLOCK
    qb = jnp.moveaxis(q.reshape(b, nb, Q_BLOCK, DIFF_HEADS, 2, HEAD_DIM), 1, 0)
    scale = HEAD_DIM ** -0.5

    def block(q_blk):
        s = jnp.einsum('bqhmd,bkhmd->bhmqk', q_blk, k_all).astype(jnp.float32) * scale
        p = jax.nn.softmax(s, axis=-1)
        a = (p[:, :, 0] - lam * p[:, :, 1]).astype(v_all.dtype)
        return jnp.einsum('bhqk,bkhv->bqhv', a, v_all)

    out = lax.map(block, qb)
    return jnp.moveaxis(out, 0, 1).reshape(b, t, DIFF_HEADS, DIFF_VDIM)


def diff_output(a, subln_w, lam_init):
    b, t = a.shape[:2]
    return (rmsnorm(a, subln_w, DIFF_SUBLN_EPS) * (1 - lam_init)).reshape(b, t, DIFF_WIDTH)


def rwkv7_scan(r, decay, k, v, kk, a, state0, reverse):
    def step(s, inp):
        r_t, w_t, k_t, v_t, kk_t, a_t = inp
        sa = jnp.einsum('bhij,bhj->bhi', s, kk_t)
        s = (s * w_t[:, :, None, :] - sa[..., None] * (kk_t * a_t)[:, :, None, :]
             + v_t[..., None] * k_t[:, :, None, :])
        return s, jnp.einsum('bhij,bhj->bhi', s, r_t)

    xs = tuple(jnp.moveaxis(z, 1, 0) for z in (r, decay, k, v, kk, a))
    s, y = lax.scan(step, state0, xs, reverse=reverse)
    return jnp.moveaxis(y, 0, 1), s


def rwkv_mixer(pb, s0_f, s0_b, rw):
    b, t, _ = pb.shape

    def heads(z):
        return z.reshape(b, t, RWKV_HEADS, HEAD_DIM)

    pb = token_shift_centred(pb, rw['mu'][0], rw['mu'][1])
    r, k, v, w_in, a_in, g_in = jnp.split(pb, RWKV_SPLITS, axis=-1)
    kk = heads(k * rw['k_k']).astype(jnp.float32)
    kk = (kk * lax.rsqrt(jnp.maximum(jnp.sum(kk * kk, axis=-1, keepdims=True), 1e-12))).astype(pb.dtype)
    r_h, v_h = heads(r), heads(v)
    ys, ks, states = [], [], []
    for d, reverse, s0 in ((0, False, s0_f), (1, True, s0_b)):
        logw = -jax.nn.softplus(-(rw['w0'][d] + jnp.tanh(w_in) @ rw['w_up'][d])) - 0.5
        decay = jnp.exp(-jnp.exp(logw.astype(jnp.float32))).astype(pb.dtype)
        a = jax.nn.sigmoid(rw['a0'][d] + a_in @ rw['a_up'][d])
        k_d = heads(k * (1 + (a - 1) * rw['k_a']))
        y, s = rwkv7_scan(r_h, heads(decay), k_d, v_h, kk, heads(a), s0, reverse)
        ys.append(y)
        ks.append(k_d)
        states.append(s)
    y = (ys[0] + ys[1]).astype(jnp.float32)
    mu = jnp.mean(y, axis=-1, keepdims=True)
    var = jnp.mean(jnp.square(y - mu), axis=-1, keepdims=True)
    yn = ((y - mu) * lax.rsqrt(var + RWKV_GN_EPS)).reshape(b, t, RWKV_WIDTH) * rw['ln_w'] + rw['ln_b']
    bonus = jnp.sum(r_h * (ks[0] + ks[1]) * rw['r_k'], axis=-1, keepdims=True) * v_h
    g = jax.nn.sigmoid(g_in) @ rw['g_up']
    out = (yn.astype(pb.dtype) + bonus.reshape(b, t, RWKV_WIDTH)) * g
    return out, states[0], states[1]


def even_mixer(h_lat, h_ctx, ang_row, ang_col, layer_idx, w_in, w_out, lam_vecs, subln_w, rw, ctx_out):
    b, t, _ = h_lat.shape
    p_lat = h_lat @ w_in
    p_ctx = h_ctx @ w_in
    lam_init = 0.8 - 0.6 * math.exp(-0.3 * layer_idx)
    lv = lam_vecs.astype(jnp.float32)
    lam = jnp.exp(jnp.sum(lv[0] * lv[1])) - jnp.exp(jnp.sum(lv[2] * lv[3])) + lam_init
    q_l, k_l, v_l = split_diff(p_lat[..., :DIFF_PROJ])
    q_c, k_c, v_c = split_diff(p_ctx[..., :DIFF_PROJ])
    q_l = rope_axial(q_l, ang_row, ang_col)
    k_l = rope_axial(k_l, ang_row, ang_col)
    k_all = jnp.concatenate([k_c, k_l], axis=1)
    v_all = jnp.concatenate([v_c, v_l], axis=1)
    att_l = diff_output(diff_attention(q_l, k_all, v_all, lam), subln_w, lam_init)
    zero_state = jnp.zeros((h_ctx.shape[0], RWKV_HEADS, HEAD_DIM, HEAD_DIM), h_lat.dtype)
    rw_c, s_f, s_b = rwkv_mixer(p_ctx[..., DIFF_PROJ:], zero_state, zero_state, rw)
    rw_l, _, _ = rwkv_mixer(p_lat[..., DIFF_PROJ:], s_f, s_b, rw)
    o_lat = jnp.concatenate([att_l, rw_l], axis=-1) @ w_out
    o_ctx = None
    if ctx_out:
        att_c = diff_output(diff_attention(q_c, k_c, v_c, lam), subln_w, lam_init)
        o_ctx = jnp.concatenate([att_c, rw_c], axis=-1) @ w_out
    return o_lat, o_ctx


def mamba_inputs(h, w_in, conv_w, conv_b):
    b, t, _ = h.shape
    p = h @ w_in
    z, xbc, dt = jnp.split(p, [SSM_INNER, SSM_INNER + SSM_CONV_CH], axis=-1)
    xbc = jax.nn.silu(dwconv_centred(xbc, conv_w, conv_b))
    xs, bm, cm = jnp.split(xbc, [SSM_INNER, SSM_INNER + SSM_GROUPS * SSM_STATE], axis=-1)
    shp = (b, t, SSM_GROUPS, SSM_HEADS_PER_GROUP, SSM_HEAD_DIM)
    return (z.reshape(shp), xs.reshape(shp), bm.reshape(b, t, SSM_GROUPS, SSM_STATE),
            cm.reshape(b, t, SSM_GROUPS, SSM_STATE), dt.reshape(b, t, 2, SSM_HEADS))


def ssm_direction(xs, dt_raw, dt_bias, a_log):
    b, t = xs.shape[:2]
    dt = jax.nn.softplus(dt_raw.astype(jnp.float32) + dt_bias)
    da = (dt * -jnp.exp(a_log.astype(jnp.float32))).reshape(b, t, SSM_GROUPS, SSM_HEADS_PER_GROUP)
    xdt = xs * dt.reshape(b, t, SSM_GROUPS, SSM_HEADS_PER_GROUP)[..., None].astype(xs.dtype)
    return xdt, da


def ssd_scan(xdt, da, bm, cm, h0):
    b, t, g, e, p = xdt.shape
    nc = t // SSM_CHUNK
    dt = xdt.dtype
    causal = jnp.tril(jnp.ones((SSM_CHUNK, SSM_CHUNK), dtype=bool))[:, :, None, None]

    def chunks(z):
        return jnp.moveaxis(z.reshape((b, nc, SSM_CHUNK) + z.shape[2:]), 1, 0)

    def step(h, inp):
        x_c, a_c, b_c, c_c = inp
        cum = jnp.cumsum(a_c, axis=1)
        seg = cum[:, :, None] - cum[:, None, :]
        decay = jnp.exp(jnp.where(causal, seg, -jnp.inf)).astype(dt)
        cb = jnp.einsum('blgn,bsgn->blsg', c_c, b_c)
        y = jnp.einsum('blsge,bsgep->blgep', cb[..., None] * decay, x_c)
        y = y + jnp.einsum('blgn,bgepn->blgep', c_c, h) * jnp.exp(cum).astype(dt)[..., None]
        to_end = jnp.exp(cum[:, -1:] - cum).astype(dt)
        h = (h * jnp.exp(cum[:, -1]).astype(dt)[..., None, None]
             + jnp.einsum('bsgn,bsge,bsgep->bgepn', b_c, to_end, x_c)).astype(h.dtype)
        return h, y

    h, ys = lax.scan(step, h0, (chunks(xdt), chunks(da), chunks(bm), chunks(cm)))
    return jnp.moveaxis(ys, 0, 1).reshape(b, t, g, e, p), h


def ssd_state(xdt, da, bm):
    cum = jnp.cumsum(da, axis=1)
    to_end = jnp.exp(cum[:, -1:] - cum).astype(xdt.dtype)
    return jnp.einsum('btgn,btge,btgep->bgepn', bm, to_end, xdt)


def mamba_out(y, z, norm_w, w_out):
    b, t = y.shape[:2]
    gy = (y * jax.nn.silu(z)).astype(jnp.float32).reshape(b, t, SSM_GROUPS, SSM_INNER // SSM_GROUPS)
    gy = gy * lax.rsqrt(jnp.mean(gy * gy, axis=-1, keepdims=True) + SSM_NORM_EPS)
    gy = gy.reshape(b, t, SSM_INNER) * norm_w.astype(jnp.float32)
    return gy.astype(y.dtype) @ w_out


def odd_mixer(h_lat, h_ctx, w_in, conv_w, conv_b, dt_bias, a_log, d_skip, norm_w, w_out, ctx_out):
    z_l, xs_l, b_l, c_l, dt_l = mamba_inputs(h_lat, w_in, conv_w, conv_b)
    z_c, xs_c, b_c, c_c, dt_c = mamba_inputs(h_ctx, w_in, conv_w, conv_b)
    d_gep = d_skip.reshape(SSM_GROUPS, SSM_HEADS_PER_GROUP, 1).astype(xs_l.dtype)
    y_l = d_gep * xs_l
    y_c = d_gep * xs_c if ctx_out else None
    h0 = jnp.zeros((h_ctx.shape[0], SSM_GROUPS, SSM_HEADS_PER_GROUP, SSM_HEAD_DIM, SSM_STATE), xs_l.dtype)
    for d in range(2):
        xdt_l, da_l = ssm_direction(xs_l, dt_l[:, :, d], dt_bias[d], a_log[d])
        xdt_c, da_c = ssm_direction(xs_c, dt_c[:, :, d], dt_bias[d], a_log[d])
        seq_l = (xdt_l, da_l, b_l, c_l)
        seq_c = (xdt_c, da_c, b_c, c_c)
        if d == 1:
            seq_l = tuple(jnp.flip(z, axis=1) for z in seq_l)
            seq_c = tuple(jnp.flip(z, axis=1) for z in seq_c)
        if ctx_out:
            yc, h_ctx_final = ssd_scan(seq_c[0], seq_c[1], seq_c[2], seq_c[3], h0)
            y_c = y_c + (jnp.flip(yc, axis=1) if d == 1 else yc)
        else:
            h_ctx_final = ssd_state(seq_c[0], seq_c[1], seq_c[2])
        yl, _ = ssd_scan(seq_l[0], seq_l[1], seq_l[2], seq_l[3], h_ctx_final)
        y_l = y_l + (jnp.flip(yl, axis=1) if d == 1 else yl)
    o_lat = mamba_out(y_l, z_l, norm_w, w_out)
    o_ctx = mamba_out(y_c, z_c, norm_w, w_out) if ctx_out else None
    return o_lat, o_ctx


def conv_ffn(h, w_up, conv_w, conv_b, w_down):
    val, gate = jnp.split(h @ w_up, 2, axis=-1)
    gate = dwconv_centred(gate, conv_w, conv_b)
    return (jax.nn.silu(gate) * val) @ w_down


def setup_inputs(seed: int = 0) -> dict:
    key = jax.random.key(seed)
    ks = iter(jax.random.split(key, 48))

    def nrm(shape, scale):
        return jax.random.normal(next(ks), shape, jnp.float32) * scale

    def unif(shape, lo, hi):
        return jax.random.uniform(next(ks), shape, jnp.float32, lo, hi)

    D = D_MODEL
    dt0 = jnp.exp(unif((N_ODD, 2, SSM_HEADS), math.log(1e-3), math.log(1e-1)))
    return {
        'x': nrm((BATCH, SEQ, D), 1.0),
        'c': nrm((BATCH, D), 1.0),
        'ctx': nrm((BATCH, CTX_LEN, D), 1.0),
        'c_ctx': nrm((D,), 1.0),
        'ada_w': nrm((DEPTH, D, 6 * D), 0.5 * D ** -0.5),
        'ada_b': nrm((DEPTH, 6 * D), 0.02),
        'norm_mix': 1.0 + nrm((DEPTH, D), 0.02),
        'norm_ffn': 1.0 + nrm((DEPTH, D), 0.02),
        'ev_w_in': nrm((N_EVEN, D, EVEN_PROJ), D ** -0.5),
        'ev_w_out': nrm((N_EVEN, MIX_WIDTH, D), MIX_WIDTH ** -0.5),
        'diff_lambda': nrm((N_EVEN, 4, HEAD_DIM), 0.1),
        'diff_subln': 1.0 + nrm((N_EVEN, DIFF_VDIM), 0.02),
        'rwkv_mu': unif((N_EVEN, 2, RWKV_PROJ), 0.0, 0.5),
        'rwkv_w0': unif((N_EVEN, 2, RWKV_WIDTH), -6.0, 0.0),
        'rwkv_w_up': nrm((N_EVEN, 2, DECAY_LORA, RWKV_WIDTH), 0.1),
        'rwkv_a0': nrm((N_EVEN, 2, RWKV_WIDTH), 0.1),
        'rwkv_a_up': nrm((N_EVEN, 2, ICLR_LORA, RWKV_WIDTH), 0.1),
        'rwkv_g_up': nrm((N_EVEN, GATE_LORA, RWKV_WIDTH), GATE_LORA ** -0.5),
        'rwkv_k_k': 0.85 + nrm((N_EVEN, RWKV_WIDTH), 0.05),
        'rwkv_k_a': 1.0 + nrm((N_EVEN, RWKV_WIDTH), 0.05),
        'rwkv_r_k': nrm((N_EVEN, RWKV_HEADS, HEAD_DIM), 0.1),
        'rwkv_ln_w': 1.0 + nrm((N_EVEN, RWKV_WIDTH), 0.02),
        'rwkv_ln_b': nrm((N_EVEN, RWKV_WIDTH), 0.02),
        'ssm_w_in': nrm((N_ODD, D, SSM_PROJ), D ** -0.5),
        'ssm_conv_w': nrm((N_ODD, SSM_CONV, SSM_CONV_CH), SSM_CONV ** -0.5),
        'ssm_conv_b': nrm((N_ODD, SSM_CONV_CH), 0.02),
        'ssm_dt_bias': dt0 + jnp.log(-jnp.expm1(-dt0)),
        'ssm_a_log': jnp.log(unif((N_ODD, 2, SSM_HEADS), 1.0, 16.0)),
        'ssm_d': 1.0 + nrm((N_ODD, SSM_HEADS), 0.1),
        'ssm_norm_w': 1.0 + nrm((N_ODD, SSM_INNER), 0.02),
        'ssm_w_out': nrm((N_ODD, SSM_INNER, D), SSM_INNER ** -0.5),
        'ffn_w_up': nrm((DEPTH, D, 2 * FFN_HIDDEN), D ** -0.5),
        'ffn_conv_w': nrm((DEPTH, FFN_CONV, FFN_HIDDEN), FFN_CONV ** -0.5),
        'ffn_conv_b': nrm((DEPTH, FFN_HIDDEN), 0.02),
        'ffn_w_down': nrm((DEPTH, FFN_HIDDEN, D), FFN_HIDDEN ** -0.5),
        'final_norm': 1.0 + nrm((D,), 0.02),
    }


def reference(x, c, ctx, c_ctx, ada_w, ada_b, norm_mix, norm_ffn, ev_w_in, ev_w_out, diff_lambda, diff_subln,
              rwkv_mu, rwkv_w0, rwkv_w_up, rwkv_a0, rwkv_a_up, rwkv_g_up, rwkv_k_k, rwkv_k_a, rwkv_r_k,
              rwkv_ln_w, rwkv_ln_b, ssm_w_in, ssm_conv_w, ssm_conv_b, ssm_dt_bias, ssm_a_log, ssm_d,
              ssm_norm_w, ssm_w_out, ffn_w_up, ffn_conv_w, ffn_conv_b, ffn_w_down, final_norm):
    rows = x.shape[1] // GRID_W
    ang_row, ang_col = axial_rope_angles(rows)
    x_lat, x_ctx = x, ctx
    for i in range(DEPTH):
        j = i // 2
        ctx_out = i < DEPTH - 1
        mod_lat = (jax.nn.silu(c) @ ada_w[i] + ada_b[i])[:, None, :]
        mod_ctx = (jax.nn.silu(c_ctx) @ ada_w[i] + ada_b[i])[None, None, :]
        sh1_l, sc1_l, g1_l, sh2_l, sc2_l, g2_l = jnp.split(mod_lat, 6, axis=-1)
        sh1_c, sc1_c, g1_c, sh2_c, sc2_c, g2_c = jnp.split(mod_ctx, 6, axis=-1)
        h_lat = modulate(rmsnorm(x_lat, norm_mix[i]), sh1_l, sc1_l)
        h_ctx = modulate(rmsnorm(x_ctx, norm_mix[i]), sh1_c, sc1_c)
        if i % 2 == 0:
            rw = {'mu': rwkv_mu[j], 'w0': rwkv_w0[j], 'w_up': rwkv_w_up[j], 'a0': rwkv_a0[j],
                  'a_up': rwkv_a_up[j], 'g_up': rwkv_g_up[j], 'k_k': rwkv_k_k[j], 'k_a': rwkv_k_a[j],
                  'r_k': rwkv_r_k[j], 'ln_w': rwkv_ln_w[j], 'ln_b': rwkv_ln_b[j]}
            o_lat, o_ctx = even_mixer(h_lat, h_ctx, ang_row, ang_col, i, ev_w_in[j], ev_w_out[j],
                                      diff_lambda[j], diff_subln[j], rw, ctx_out)
        else:
            o_lat, o_ctx = odd_mixer(h_lat, h_ctx, ssm_w_in[j], ssm_conv_w[j], ssm_conv_b[j], ssm_dt_bias[j],
                                     ssm_a_log[j], ssm_d[j], ssm_norm_w[j], ssm_w_out[j], ctx_out)
        x_lat = x_lat + g1_l * o_lat
        h_lat = modulate(rmsnorm(x_lat, norm_ffn[i]), sh2_l, sc2_l)
        x_lat = x_lat + g2_l * conv_ffn(h_lat, ffn_w_up[i], ffn_conv_w[i], ffn_conv_b[i], ffn_w_down[i])
        if ctx_out:
            x_ctx = x_ctx + g1_c * o_ctx
            h_ctx = modulate(rmsnorm(x_ctx, norm_ffn[i]), sh2_c, sc2_c)
            x_ctx = x_ctx + g2_c * conv_ffn(h_ctx, ffn_w_up[i], ffn_conv_w[i], ffn_conv_b[i], ffn_w_down[i])
    return rmsnorm(x_lat, final_norm)
```

```python
import functools
import math

import jax
import jax.numpy as jnp
from jax import lax
from jax.experimental import pallas as pl
from jax.experimental.pallas import tpu as pltpu

F32 = jnp.float32
BF16 = jnp.bfloat16

NORM_EPS = 1e-6
GRID_W = 64
HEAD_DIM = 64
DIFF_HEADS = 4
DIFF_WIDTH = 512
DIFF_SUBLN_EPS = 1e-5
ROPE_BASE = 10000.0
RWKV_HEADS = 8
RWKV_WIDTH = 512
RWKV_GN_EPS = 64e-5
SSM_NORM_EPS = 1e-5
SSM_HEADS = 32
SSM_GROUPS = 4
SSM_STATE = 128

VMEM_LIMIT_BYTES = 56 * 1024 * 1024
HALO = 8


def _params(*sem):
    return pltpu.CompilerParams(dimension_semantics=sem, vmem_limit_bytes=VMEM_LIMIT_BYTES)


def _dot(a, b):
    return jnp.dot(a, b, preferred_element_type=F32)


def _dot_nt(a, b):
    return lax.dot_general(a, b, (((1,), (1,)), ((), ())), preferred_element_type=F32)


def _split3(a):
    hi = a.astype(BF16)
    r1 = a - hi.astype(F32)
    mid = r1.astype(BF16)
    lo = (r1 - mid.astype(F32)).astype(BF16)
    return hi, mid, lo


def _dot_exact_rhs(a, b_bf16):
    hi, mid, lo = _split3(a)
    return _dot(hi, b_bf16) + _dot(mid, b_bf16) + _dot(lo, b_bf16)


def _dot_exact_lhs(a_bf16, b):
    hi, mid, lo = _split3(b)
    return _dot(a_bf16, hi) + _dot(a_bf16, mid) + _dot(a_bf16, lo)


def _sigmoid(x):
    return 1.0 / (1.0 + jnp.exp(-x))


def _silu(x):
    return x * _sigmoid(x)


def _softplus(x):
    return jnp.maximum(x, 0.0) + jnp.log(1.0 + jnp.exp(-jnp.abs(x)))


def _row_tile(t, want):
    tm = min(t, want)
    assert t % tm == 0 and tm % HALO == 0
    return tm


def _mod_kernel(c_ref, w_ref, b_ref, o_ref):
    o_ref[0] = jnp.dot(_silu(c_ref[...]), w_ref[0], preferred_element_type=F32,
                       precision=lax.Precision.HIGHEST) + b_ref[0]


def _modulation(cvec, ada_w, ada_b):
    depth, d, n = ada_w.shape
    tn = 1536
    assert n % tn == 0
    return pl.pallas_call(
        _mod_kernel,
        grid=(depth, n // tn),
        in_specs=[pl.BlockSpec((8, d), lambda l, j: (0, 0)),
                  pl.BlockSpec((1, d, tn), lambda l, j: (l, 0, j)),
                  pl.BlockSpec((1, 1, tn), lambda l, j: (l, 0, j))],
        out_specs=pl.BlockSpec((1, 8, tn), lambda l, j: (l, 0, j)),
        out_shape=jax.ShapeDtypeStruct((depth, 8, n), F32),
        compiler_params=_params("parallel", "parallel"),
    )(cvec, ada_w, ada_b.reshape(depth, 1, n))


def _nmm_kernel(x_ref, nw_ref, mod_ref, *rest, k_shift, k_scale, n_out):
    w_refs, o_refs = rest[:n_out], rest[n_out:]
    x = x_ref[0]
    h = x * lax.rsqrt(jnp.mean(x * x, axis=-1, keepdims=True) + NORM_EPS) * nw_ref[...]
    h = h * (1.0 + mod_ref[0, k_scale:k_scale + 1, :]) + mod_ref[0, k_shift:k_shift + 1, :]
    hb = h.astype(BF16)
    for w_ref, o_ref in zip(w_refs, o_refs):
        o_ref[0] = _dot(hb, w_ref[...]).astype(o_ref.dtype)


def _norm_mod_matmul(x, norm_w, mod, mod_row, k_shift, k_scale, weights, tm_want):
    b, t, d = x.shape
    tm = _row_tile(t, tm_want)
    n_out = len(weights)
    in_specs = [pl.BlockSpec((1, tm, d), lambda bi, i: (bi, i, 0)),
                pl.BlockSpec((1, d), lambda bi, i: (0, 0)),
                pl.BlockSpec((1, 6, d), lambda bi, i: (mod_row(bi), 0, 0))]
    in_specs += [pl.BlockSpec(w.shape, lambda bi, i: (0, 0)) for w in weights]
    out_specs = [pl.BlockSpec((1, tm, w.shape[1]), lambda bi, i: (bi, i, 0)) for w in weights]
    out_shape = [jax.ShapeDtypeStruct((b, t, w.shape[1]), F32) for w in weights]
    return pl.pallas_call(
        functools.partial(_nmm_kernel, k_shift=k_shift, k_scale=k_scale, n_out=n_out),
        grid=(b, t // tm),
        in_specs=in_specs, out_specs=out_specs, out_shape=out_shape,
        compiler_params=_params("parallel", "parallel"),
    )(x, norm_w.reshape(1, d), mod, *weights)


def _halo_specs(tm, t, width, col_block=0):
    per = tm // HALO
    last = t // HALO - 1
    return [pl.BlockSpec((1, tm, width), lambda bi, i: (bi, i, col_block)),
            pl.BlockSpec((1, HALO, width), lambda bi, i: (bi, jnp.maximum(i * per - 1, 0), col_block)),
            pl.BlockSpec((1, HALO, width), lambda bi, i: (bi, jnp.minimum((i + 1) * per, last), col_block))]


def _fill_ext(ext_ref, cur_ref, prev_ref, next_ref):
    i = pl.program_id(1)
    tm = cur_ref.shape[1]
    ext_ref[0:HALO, :] = jnp.where(i > 0, prev_ref[0], 0.0)
    ext_ref[HALO:HALO + tm, :] = cur_ref[0]
    ext_ref[HALO + tm:, :] = jnp.where(i < pl.num_programs(1) - 1, next_ref[0], 0.0)


def _ffn_down_kernel(val_ref, gate_ref, gprev_ref, gnext_ref, cw_ref, cb_ref, wd_ref, x_ref, mod_ref,
                     *rest, k_gate, final):
    if final:
        fn_ref, o_ref, ext_ref = rest
    else:
        o_ref, ext_ref = rest
    tm = gate_ref.shape[1]
    _fill_ext(ext_ref, gate_ref, gprev_ref, gnext_ref)
    g = cb_ref[...]
    for k in range(3):
        g = g + cw_ref[k:k + 1, :] * ext_ref[HALO - 1 + k:HALO - 1 + k + tm, :]
    act = (_silu(g) * val_ref[0]).astype(BF16)
    y = x_ref[0] + mod_ref[0, k_gate:k_gate + 1, :] * _dot(act, wd_ref[...])
    if final:
        y = y * lax.rsqrt(jnp.mean(y * y, axis=-1, keepdims=True) + NORM_EPS) * fn_ref[...]
    o_ref[0] = y


def _ffn_down(val, gate, conv_w, conv_b, w_down, x, mod, mod_row, k_gate, final_norm, tm_want):
    b, t, hdim = val.shape
    d = x.shape[-1]
    tm = _row_tile(t, tm_want)
    final = final_norm is not None
    in_specs = [pl.BlockSpec((1, tm, hdim), lambda bi, i: (bi, i, 0))]
    in_specs += _halo_specs(tm, t, hdim)
    in_specs += [pl.BlockSpec((3, hdim), lambda bi, i: (0, 0)),
                 pl.BlockSpec((1, hdim), lambda bi, i: (0, 0)),
                 pl.BlockSpec((hdim, d), lambda bi, i: (0, 0)),
                 pl.BlockSpec((1, tm, d), lambda bi, i: (bi, i, 0)),
                 pl.BlockSpec((1, 6, d), lambda bi, i: (mod_row(bi), 0, 0))]
    args = [val, gate, gate, gate, conv_w, conv_b.reshape(1, hdim), w_down, x, mod]
    if final:
        in_specs.append(pl.BlockSpec((1, d), lambda bi, i: (0, 0)))
        args.append(final_norm.reshape(1, d))
    return pl.pallas_call(
        functools.partial(_ffn_down_kernel, k_gate=k_gate, final=final),
        grid=(b, t // tm),
        in_specs=in_specs,
        out_specs=pl.BlockSpec((1, tm, d), lambda bi, i: (bi, i, 0)),
        out_shape=jax.ShapeDtypeStruct((b, t, d), F32),
        scratch_shapes=[pltpu.VMEM((tm + 2 * HALO, hdim), F32)],
        compiler_params=_params("parallel", "parallel"),
    )(*args)


def _rope_tables(t):
    pos = jnp.arange(t)
    row = (pos // GRID_W).astype(F32)
    col = (pos % GRID_W).astype(F32)
    half = HEAD_DIM // 4
    inv = ROPE_BASE ** (-jnp.arange(0, 2 * half, 2, dtype=F32) / (2 * half))
    ang_r, ang_c = row[:, None] * inv, col[:, None] * inv
    ang = jnp.concatenate([ang_r, ang_r, ang_c, ang_c] * 2, axis=-1)
    return jnp.cos(ang), jnp.sin(ang)


def _qkv_prep_kernel(q_ref, k_ref, v_ref, cos_ref, sin_ref, qz_ref, kr_ref, vt_ref, *, q_scale):
    cos, sin = cos_ref[...], sin_ref[...]
    lane = lax.broadcasted_iota(jnp.int32, cos.shape, 1)
    first_half = (lane % 32) < 16

    def rope(x):
        rot = jnp.where(first_half, -pltpu.roll(x, 112, 1), pltpu.roll(x, 16, 1))
        return x * cos + rot * sin

    for h in range(DIFF_HEADS):
        sl = slice(h * 128, (h + 1) * 128)
        q = rope(q_ref[0, :, sl]) * q_scale
        for m in range(2):
            in_map = (lane >= m * HEAD_DIM) & (lane < (m + 1) * HEAD_DIM)
            qz_ref[0, :, (2 * h + m) * 128:(2 * h + m + 1) * 128] = jnp.where(in_map, q, 0.0).astype(BF16)
        kr_ref[0, :, sl] = rope(k_ref[0, :, sl]).astype(BF16)
        vt_ref[0, h] = v_ref[0, :, sl].T.astype(BF16)


def _qkv_prep(q, k, v, cos, sin, tm_want):
    b, t, w = q.shape
    tm = _row_tile(t, tm_want)
    q_scale = HEAD_DIM ** -0.5 * math.log2(math.e)
    row_spec = pl.BlockSpec((1, tm, w), lambda bi, i: (bi, i, 0))
    tab_spec = pl.BlockSpec((tm, 128), lambda bi, i: (i, 0))
    return pl.pallas_call(
        functools.partial(_qkv_prep_kernel, q_scale=q_scale),
        grid=(b, t // tm),
        in_specs=[row_spec, row_spec, row_spec, tab_spec, tab_spec],
        out_specs=[pl.BlockSpec((1, tm, 2 * w), lambda bi, i: (bi, i, 0)),
                   row_spec,
                   pl.BlockSpec((1, DIFF_HEADS, 128, tm), lambda bi, i: (bi, 0, 0, i))],
        out_shape=[jax.ShapeDtypeStruct((b, t, 2 * w), BF16),
                   jax.ShapeDtypeStruct((b, t, w), BF16),
                   jax.ShapeDtypeStruct((b, DIFF_HEADS, 128, t), BF16)],
        compiler_params=_params("parallel", "parallel"),
    )(q, k, v, cos, sin)


def _attn_kernel(lv_ref, sw_ref, k_ref, q_ref, vt_ref, o_ref, m_ref, l_ref, acc_ref, *, lam_init):
    j = pl.program_id(3)

    @pl.when(j == 0)
    def _():
        m_ref[...] = jnp.full(m_ref.shape, -jnp.inf, F32)
        l_ref[...] = jnp.zeros(l_ref.shape, F32)
        acc_ref[...] = jnp.zeros(acc_ref.shape, F32)

    k = k_ref[0]
    vt = vt_ref[0, 0]
    for m in range(2):
        s = _dot_nt(k, q_ref[0, :, m * 128:(m + 1) * 128])
        m_old = m_ref[m:m + 1, :]
        m_new = jnp.maximum(m_old, jnp.max(s, axis=0, keepdims=True))
        alpha = jnp.exp2(m_old - m_new)
        p = jnp.exp2(s - m_new)
        l_ref[m:m + 1, :] = alpha * l_ref[m:m + 1, :] + jnp.sum(p, axis=0, keepdims=True)
        acc_ref[m] = alpha * acc_ref[m] + _dot(vt, p.astype(BF16))
        m_ref[m:m + 1, :] = m_new

    @pl.when(j == pl.num_programs(3) - 1)
    def _():
        lv = lv_ref[...]
        lam = (jnp.exp(jnp.sum(lv[0:1] * lv[1:2], axis=-1, keepdims=True))
               - jnp.exp(jnp.sum(lv[2:3] * lv[3:4], axis=-1, keepdims=True)) + lam_init)
        a = acc_ref[0] / l_ref[0:1, :] - lam * (acc_ref[1] / l_ref[1:2, :])
        y = a * lax.rsqrt(jnp.mean(a * a, axis=0, keepdims=True) + DIFF_SUBLN_EPS)
        o_ref[0] = (y * (sw_ref[...] * (1.0 - lam_init))).T


def _pick_tile(n, want, mult=128):
    best = None
    for c in range(mult, min(n, want) + 1, mult):
        if n % c == 0:
            best = c
    assert best is not None, (n, want)
    return best


def _diff_attention(qz, kr, vt, lam_vecs, subln_w, lam_init, tq_want, tk_want):
    b, t, _ = qz.shape
    lk = kr.shape[1]
    tq = _pick_tile(t, tq_want)
    tk = _pick_tile(lk, tk_want)
    return pl.pallas_call(
        functools.partial(_attn_kernel, lam_init=lam_init),
        grid=(b, DIFF_HEADS, t // tq, lk // tk),
        in_specs=[pl.BlockSpec((4, HEAD_DIM), lambda bi, h, i, j: (0, 0)),
                  pl.BlockSpec((128, 1), lambda bi, h, i, j: (0, 0)),
                  pl.BlockSpec((1, tk, 128), lambda bi, h, i, j: (bi, j, h)),
                  pl.BlockSpec((1, tq, 256), lambda bi, h, i, j: (bi, i, h)),
                  pl.BlockSpec((1, 1, 128, tk), lambda bi, h, i, j: (bi, h, 0, j))],
        out_specs=pl.BlockSpec((1, tq, 128), lambda bi, h, i, j: (bi, i, h)),
        out_shape=jax.ShapeDtypeStruct((b, t, DIFF_WIDTH), F32),
        scratch_shapes=[pltpu.VMEM((8, tq), F32), pltpu.VMEM((8, tq), F32), pltpu.VMEM((2, 128, tq), F32)],
        compiler_params=_params("parallel", "parallel", "parallel", "arbitrary"),
    )(lam_vecs, subln_w.reshape(128, 1), kr, qz, vt)


def _head_ones(width, head, scale=1.0):
    idx = jnp.arange(width) // head
    return jnp.where(idx[:, None] == idx[None, :], scale, 0.0).astype(BF16)


def _rwkv_prep_kernel(cur_ref, prev_ref, next_ref, mu_ref, w0_ref, a0_ref, wl_ref, gup_ref, kk_w_ref, ka_ref,
                      rk_ref, j_ref, r_ref, v_ref, kk_ref, g_ref, bonus_ref, ld_ref, kd_ref, bd_ref, ext_ref):
    tm = cur_ref.shape[1]
    w = RWKV_WIDTH
    _fill_ext(ext_ref, cur_ref, prev_ref, next_ref)
    p = cur_ref[0]
    xs = (p + mu_ref[0:1, :] * (ext_ref[HALO - 1:HALO - 1 + tm, :] - p)
          + mu_ref[1:2, :] * (ext_ref[HALO + 1:HALO + 1 + tm, :] - p))
    r, k, v = xs[:, 0:w], xs[:, w:2 * w], xs[:, 2 * w:3 * w]
    lora_in = xs[:, 3 * w:3 * w + 128]
    g_in = xs[:, 3 * w + 128:3 * w + 256]
    lane = lax.broadcasted_iota(jnp.int32, lora_in.shape, 1)
    lora_in = jnp.where(lane < 64, jnp.tanh(lora_in), lora_in).astype(BF16)
    kkr = k * kk_w_ref[...]
    ss = _dot_exact_rhs(kkr * kkr, j_ref[...])
    kk = kkr * lax.rsqrt(jnp.maximum(ss, 1e-12))
    r_ref[0], v_ref[0], kk_ref[0] = r, v, kk
    g_ref[0] = _dot(_sigmoid(g_in).astype(BF16), gup_ref[...])
    ksum = jnp.zeros_like(k)
    for d in range(2):
        lo = _dot(lora_in, wl_ref[d])
        logw = -_softplus(-(w0_ref[d:d + 1, :] + lo[:, :w])) - 0.5
        ld_ref[0, d] = -jnp.exp(logw)
        a = _sigmoid(a0_ref[d:d + 1, :] + lo[:, w:])
        kd = k * (1.0 + (a - 1.0) * ka_ref[...])
        kd_ref[0, d] = kd
        bd_ref[0, d] = kk * a
        ksum = ksum + kd
    bonus_ref[0] = _dot_exact_rhs(r * ksum * rk_ref[...], j_ref[...]) * v


def _rwkv_prep(pr, rw, tm_want):
    b, t, width = pr.shape
    w = RWKV_WIDTH
    tm = _row_tile(t, tm_want)
    zeros = jnp.zeros((64, w), F32)
    wl = jnp.stack([jnp.concatenate([jnp.concatenate([rw['w_up'][d], zeros], axis=1),
                                     jnp.concatenate([zeros, rw['a_up'][d]], axis=1)], axis=0)
                    for d in range(2)]).astype(BF16)
    const = lambda shape: pl.BlockSpec(shape, lambda bi, i: (0,) * len(shape))
    tok = pl.BlockSpec((1, tm, w), lambda bi, i: (bi, i, 0))
    tok2 = pl.BlockSpec((1, 2, tm, w), lambda bi, i: (bi, 0, i, 0))
    return pl.pallas_call(
        _rwkv_prep_kernel,
        grid=(b, t // tm),
        in_specs=_halo_specs(tm, t, width) + [
            const((2, width)), const((2, w)), const((2, w)), const((2, 128, 2 * w)), const((128, w)),
            const((1, w)), const((1, w)), const((1, w)), const((w, w))],
        out_specs=[tok] * 5 + [tok2] * 3,
        out_shape=[jax.ShapeDtypeStruct((b, t, w), F32)] * 5 + [jax.ShapeDtypeStruct((b, 2, t, w), F32)] * 3,
        scratch_shapes=[pltpu.VMEM((tm + 2 * HALO, width), F32)],
        compiler_params=_params("parallel", "parallel"),
    )(pr, pr, pr, rw['mu'], rw['w0'], rw['a0'], wl, rw['g_up'].astype(BF16), rw['k_k'].reshape(1, w),
      rw['k_a'].reshape(1, w), rw['r_k'].reshape(1, w), _head_ones(w, HEAD_DIM))


RWKV_CHUNK = 64
RWKV_GROUP = 4 * HEAD_DIM


def _bdiag(z, blockmask):
    zt = jnp.concatenate([z] * 4, axis=0)
    return jnp.where(blockmask, zt, jnp.zeros_like(zt))


def _rwkv_chunk_kernel(s0_ref, r0_ref, v0_ref, kk0_ref, ld0_ref, kd0_ref, bd0_ref,
                       r1_ref, v1_ref, kk1_ref, ld1_ref, kd1_ref, bd1_ref, y_ref0, y_ref1, sout_ref, s_ref):
    c = pl.program_id(1)
    L, G = RWKV_CHUNK, RWKV_GROUP

    @pl.when(c == 0)
    def _():
        s_ref[...] = s0_ref[0]

    row = lax.broadcasted_iota(jnp.int32, (L, G), 0)
    col = lax.broadcasted_iota(jnp.int32, (L, G), 1) % L
    rr = lax.broadcasted_iota(jnp.int32, (G, G), 0)
    cc = lax.broadcasted_iota(jnp.int32, (G, G), 1)
    blockmask = (rr // HEAD_DIM) == (cc // HEAD_DIM)
    eye = rr == cc
    tr = lax.broadcasted_iota(jnp.int32, (L, L), 0)
    tc = lax.broadcasted_iota(jnp.int32, (L, L), 1)

    dirs = ((r0_ref, v0_ref, kk0_ref, ld0_ref, kd0_ref, bd0_ref, y_ref0),
            (r1_ref, v1_ref, kk1_ref, ld1_ref, kd1_ref, bd1_ref, y_ref1))
    for d, (r_ref, v_ref, kk_ref, ld_ref, kd_ref, bd_ref, y_ref) in enumerate(dirs):
        incl = (col <= row) if d == 0 else (col >= row)
        strict = (col < row) if d == 0 else (col > row)
        cum_mat = jnp.where((tc <= tr) if d == 0 else (tc >= tr), 1.0, 0.0).astype(BF16)
        end = L - 1 if d == 0 else 0
        for g in range(2):
            gs = slice(g * G, (g + 1) * G)
            r, v, kk = r_ref[0, :, gs], v_ref[0, :, gs], kk_ref[0, :, gs]
            ld, kd, bd = ld_ref[0, 0, :, gs], kd_ref[0, 0, :, gs], bd_ref[0, 0, :, gs]
            cum = _dot_exact_lhs(cum_mat, ld)
            cend = cum[end:end + 1, :]
            e_in, e_out, e_end = jnp.exp(cum), jnp.exp(-cum), jnp.exp(cend - cum)
            at = -kk * jnp.exp(cum - ld)
            rt = r * e_in
            bt = _bdiag((bd * e_out).astype(BF16), blockmask)
            kt = _bdiag((kd * e_out).astype(BF16), blockmask)
            vb = _bdiag(v.astype(BF16), blockmask)
            lhs = jnp.concatenate([at, rt], axis=0).astype(BF16)
            ab = _dot_nt(lhs, bt)
            ak = _dot_nt(lhs, kt)
            a_ab = jnp.where(strict, ab[:L], 0.0)
            a_ak = jnp.where(strict, ak[:L], 0.0).astype(BF16)
            a_rb = jnp.where(incl, ab[L:], 0.0).astype(BF16)
            a_rk = jnp.where(incl, ak[L:], 0.0).astype(BF16)
            xa, xv, n = at, _dot(a_ak, vb), a_ab
            steps = L.bit_length() - 1
            for it in range(steps):
                nb = n.astype(BF16)
                parts = [_bdiag(xa.astype(BF16), blockmask), _bdiag(xv.astype(BF16), blockmask)]
                if it < steps - 1:
                    parts.append(_bdiag(nb, blockmask))
                res = _dot(nb, jnp.concatenate(parts, axis=1))
                xa, xv = xa + res[:, :G], xv + res[:, G:2 * G]
                if it < steps - 1:
                    n = res[:, 2 * G:]
            wb = _bdiag(xa.astype(BF16), blockmask)
            ub = _bdiag(xv.astype(BF16), blockmask)
            rw_ = rt + _dot(a_rb, wb)
            y0 = _dot(a_rb, ub) + _dot(a_rk, vb)
            bh = (bd * e_end).astype(BF16).T
            kh = (kd * e_end).astype(BF16).T
            m_full = _dot(bh, xa.astype(BF16)) + jnp.where(eye, jnp.exp(cend), 0.0)
            g_full = _dot(bh, xv.astype(BF16)) + _dot(kh, v.astype(BF16))
            s_old = s_ref[d, g]
            sb = s_old.astype(BF16)
            y_ref[0, :, gs] = _dot(rw_.astype(BF16), sb) + y0
            s_ref[d, g] = jnp.where(blockmask, _dot(m_full.astype(BF16), sb) + g_full, 0.0)

    @pl.when(c == pl.num_programs(1) - 1)
    def _():
        sout_ref[0] = s_ref[...]


def _rwkv_scan(s0, r, v, kk, ld, kd, bd):
    b, t, w = r.shape
    L = RWKV_CHUNK
    assert t % L == 0
    nc = t // L
    fwd = pl.BlockSpec((1, L, w), lambda bi, c: (bi, c, 0))
    bwd = pl.BlockSpec((1, L, w), lambda bi, c: (bi, nc - 1 - c, 0))
    fwd2 = pl.BlockSpec((1, 1, L, w), lambda bi, c: (bi, 0, c, 0))
    bwd2 = pl.BlockSpec((1, 1, L, w), lambda bi, c: (bi, 1, nc - 1 - c, 0))
    st = pl.BlockSpec((1, 2, 2, RWKV_GROUP, RWKV_GROUP), lambda bi, c: (bi, 0, 0, 0, 0))
    y0, y1, s_out = pl.pallas_call(
        _rwkv_chunk_kernel,
        grid=(b, nc),
        in_specs=[st, fwd, fwd, fwd, fwd2, fwd2, fwd2, bwd, bwd, bwd, bwd2, bwd2, bwd2],
        out_specs=[fwd, bwd, st],
        out_shape=[jax.ShapeDtypeStruct((b, t, w), F32), jax.ShapeDtypeStruct((b, t, w), F32),
                   jax.ShapeDtypeStruct(s0.shape, F32)],
        scratch_shapes=[pltpu.VMEM((2, 2, RWKV_GROUP, RWKV_GROUP), F32)],
        compiler_params=_params("parallel", "arbitrary"),
    )(s0, r, v, kk, ld, kd, bd, r, v, kk, ld, kd, bd)
    return y0, y1, s_out


def _even_out_kernel(att_ref, y0_ref, y1_ref, g_ref, bonus_ref, lnw_ref, lnb_ref, jm_ref, wo_ref, x_ref,
                     mod_ref, o_ref, *, k_gate):
    y = y0_ref[0] + y1_ref[0]
    mu = _dot_exact_rhs(y, jm_ref[...])
    yc = y - mu
    var = _dot_exact_rhs(yc * yc, jm_ref[...])
    yn = yc * lax.rsqrt(var + RWKV_GN_EPS) * lnw_ref[...] + lnb_ref[...]
    rw_out = ((yn + bonus_ref[0]) * g_ref[0]).astype(BF16)
    w = att_ref.shape[2]
    o = _dot(att_ref[0].astype(BF16), wo_ref[0:w, :]) + _dot(rw_out, wo_ref[w:, :])
    o_ref[0] = x_ref[0] + mod_ref[0, k_gate:k_gate + 1, :] * o


def _even_out(att, y0, y1, g, bonus, ln_w, ln_b, w_out, x, mod, mod_row, k_gate, tm_want):
    b, t, w = att.shape
    d = x.shape[-1]
    tm = _row_tile(t, tm_want)
    tok = pl.BlockSpec((1, tm, w), lambda bi, i: (bi, i, 0))
    const = lambda shape: pl.BlockSpec(shape, lambda bi, i: (0,) * len(shape))
    return pl.pallas_call(
        functools.partial(_even_out_kernel, k_gate=k_gate),
        grid=(b, t // tm),
        in_specs=[tok] * 5 + [const((1, w)), const((1, w)), const((w, w)), const(w_out.shape),
                              pl.BlockSpec((1, tm, d), lambda bi, i: (bi, i, 0)),
                              pl.BlockSpec((1, 6, d), lambda bi, i: (mod_row(bi), 0, 0))],
        out_specs=pl.BlockSpec((1, tm, d), lambda bi, i: (bi, i, 0)),
        out_shape=jax.ShapeDtypeStruct((b, t, d), F32),
        compiler_params=_params("parallel", "parallel"),
    )(att, y0, y1, g, bonus, ln_w.reshape(1, w), ln_b.reshape(1, w),
      _head_ones(w, HEAD_DIM, 1.0 / HEAD_DIM), w_out, x, mod)


SSM_CONV = 5
SSM_CHUNK = 128


def _ssm_prep_kernel(cur_ref, prev_ref, next_ref, dt_ref, cw_ref, cb_ref, dtb_ref, nega_ref, xbc_ref, dtda_ref,
                     ext_ref):
    tm = cur_ref.shape[1]
    _fill_ext(ext_ref, cur_ref, prev_ref, next_ref)
    acc = cb_ref[...]
    pad = SSM_CONV // 2
    for k in range(SSM_CONV):
        acc = acc + cw_ref[k:k + 1, :] * ext_ref[HALO - pad + k:HALO - pad + k + tm, :]
    xbc_ref[0] = _silu(acc)
    dt = _softplus(dt_ref[0] + dtb_ref[...])
    lane = lax.broadcasted_iota(jnp.int32, dt.shape, 1)
    dtda_ref[0] = jnp.where(lane < 64, dt, pltpu.roll(dt, 64, 1) * nega_ref[...])


def _ssm_prep(xbc_raw, dt_raw, conv_w, conv_b, dt_bias, a_log, tm_want):
    b, t, width = xbc_raw.shape
    tm = _row_tile(t, tm_want)
    zeros = jnp.zeros((1, 64), F32)
    dtb = jnp.concatenate([dt_bias.reshape(1, 64), zeros], axis=1)
    nega = jnp.concatenate([zeros, -jnp.exp(a_log.reshape(1, 64))], axis=1)
    const = lambda shape: pl.BlockSpec(shape, lambda bi, i: (0,) * len(shape))
    return pl.pallas_call(
        _ssm_prep_kernel,
        grid=(b, t // tm),
        in_specs=_halo_specs(tm, t, width) + [pl.BlockSpec((1, tm, 128), lambda bi, i: (bi, i, 0)),
                                              const((SSM_CONV, width)), const((1, width)),
                                              const((1, 128)), const((1, 128))],
        out_specs=[pl.BlockSpec((1, tm, width), lambda bi, i: (bi, i, 0)),
                   pl.BlockSpec((1, tm, 128), lambda bi, i: (bi, i, 0))],
        out_shape=[jax.ShapeDtypeStruct((b, t, width), F32), jax.ShapeDtypeStruct((b, t, 128), F32)],
        scratch_shapes=[pltpu.VMEM((tm + 2 * HALO, width), F32)],
        compiler_params=_params("parallel", "parallel"),
    )(xbc_raw, xbc_raw, xbc_raw, dt_raw, conv_w, conv_b.reshape(1, width), dtb, nega)


def _ssd_kernel(h0_ref, xs0_ref, b0_ref, c0_ref, dd0_ref, xs1_ref, b1_ref, c1_ref, dd1_ref,
                y0_ref, y1_ref, hout_ref, h_ref):
    c = pl.program_id(1)
    L = SSM_CHUNK

    @pl.when(c == 0)
    def _():
        h_ref[...] = h0_ref[0]

    tr = lax.broadcasted_iota(jnp.int32, (L, L), 0)
    tc = lax.broadcasted_iota(jnp.int32, (L, L), 1)
    lane = lax.broadcasted_iota(jnp.int32, (L, 128), 1)
    dirs = ((xs0_ref, b0_ref, c0_ref, dd0_ref, y0_ref), (xs1_ref, b1_ref, c1_ref, dd1_ref, y1_ref))
    for d, (xs_ref, b_ref, c_ref, dd_ref, y_ref) in enumerate(dirs):
        mask = (tc <= tr) if d == 0 else (tc >= tr)
        end = L - 1 if d == 0 else 0
        dtda = dd_ref[0]
        dtda_t = dtda.T
        cum = _dot_exact_lhs(jnp.where(mask, 1.0, 0.0).astype(BF16), dtda)
        cum_t = cum.T
        for g in range(SSM_GROUPS):
            bg = b_ref[0, :, g * 128:(g + 1) * 128]
            cg = c_ref[0, :, g * 128:(g + 1) * 128]
            cb = _dot_nt(cg.astype(BF16), bg.astype(BF16))
            bg_t = bg.T
            for pr in range(4):
                pidx = g * 4 + pr
                xs_pair = xs_ref[0, :, pidx * 128:(pidx + 1) * 128].astype(BF16)
                hp = h_ref[d, pidx]
                rhs = jnp.concatenate([xs_pair, hp.astype(BF16)], axis=0)
                res, hnew = [], []
                for e in range(2):
                    head = g * 8 + 2 * pr + e
                    li = 64 + d * 32 + head
                    cum_col, cum_row = cum[:, li:li + 1], cum_t[li:li + 1, :]
                    dt_row = dtda_t[d * 32 + head:d * 32 + head + 1, :]
                    dec = jnp.exp(jnp.where(mask, cum_col - cum_row, -jnp.inf))
                    lhs = jnp.concatenate([cb * dec * dt_row, jnp.exp(cum_col) * cg], axis=1).astype(BF16)
                    res.append(_dot(lhs, rhs))
                    cend = cum_row[:, end:end + 1]
                    w_row = jnp.exp(cend - cum_row) * dt_row
                    hnew.append(jnp.exp(cend) * hp + _dot((bg_t * w_row).astype(BF16), xs_pair))
                y_ref[0, :, pidx * 128:(pidx + 1) * 128] = jnp.where(lane < 64, res[0], res[1])
                h_ref[d, pidx] = jnp.where(lane < 64, hnew[0], hnew[1])

    @pl.when(c == pl.num_programs(1) - 1)
    def _():
        hout_ref[0] = h_ref[...]


def _ssd_scan(h0, xbc, dtda):
    b, t, _ = xbc.shape
    L = SSM_CHUNK
    assert t % L == 0
    nc = t // L
    inner = SSM_HEADS * HEAD_DIM
    gw = SSM_GROUPS * SSM_STATE
    specs = []
    for idx in (lambda c: c, lambda c: nc - 1 - c):
        specs += [pl.BlockSpec((1, L, inner), lambda bi, c, idx=idx: (bi, idx(c), 0)),
                  pl.BlockSpec((1, L, gw), lambda bi, c, idx=idx: (bi, idx(c), inner // gw)),
                  pl.BlockSpec((1, L, gw), lambda bi, c, idx=idx: (bi, idx(c), inner // gw + 1)),
                  pl.BlockSpec((1, L, 128), lambda bi, c, idx=idx: (bi, idx(c), 0))]
    st = pl.BlockSpec((1,) + h0.shape[1:], lambda bi, c: (bi, 0, 0, 0, 0))
    return pl.pallas_call(
        _ssd_kernel,
        grid=(b, nc),
        in_specs=[st] + specs,
        out_specs=[specs[0], specs[4], st],
        out_shape=[jax.ShapeDtypeStruct((b, t, inner), F32), jax.ShapeDtypeStruct((b, t, inner), F32),
                   jax.ShapeDtypeStruct(h0.shape, F32)],
        scratch_shapes=[pltpu.VMEM(h0.shape[1:], F32)],
        compiler_params=_params("parallel", "arbitrary"),
    )(h0, xbc, xbc, xbc, dtda, xbc, xbc, xbc, dtda)


def _ssm_out_kernel(y0_ref, y1_ref, xs_ref, z_ref, dsk_ref, nw_ref, wo_ref, x_ref, mod_ref, o_ref, *, k_gate):
    gy = (y0_ref[0] + y1_ref[0] + dsk_ref[...] * xs_ref[0]) * _silu(z_ref[0])
    gw = gy.shape[1] // SSM_GROUPS
    parts = []
    for g in range(SSM_GROUPS):
        part = gy[:, g * gw:(g + 1) * gw]
        parts.append(part * lax.rsqrt(jnp.mean(part * part, axis=-1, keepdims=True) + SSM_NORM_EPS))
    gyn = (jnp.concatenate(parts, axis=1) * nw_ref[...]).astype(BF16)
    o_ref[0] = x_ref[0] + mod_ref[0, k_gate:k_gate + 1, :] * _dot(gyn, wo_ref[...])


def _ssm_out(y0, y1, xbc, z, d_skip, norm_w, w_out, x, mod, mod_row, k_gate, tm_want):
    b, t, inner = y0.shape
    d = x.shape[-1]
    tm = _row_tile(t, tm_want)
    tok = pl.BlockSpec((1, tm, inner), lambda bi, i: (bi, i, 0))
    const = lambda shape: pl.BlockSpec(shape, lambda bi, i: (0,) * len(shape))
    return pl.pallas_call(
        functools.partial(_ssm_out_kernel, k_gate=k_gate),
        grid=(b, t // tm),
        in_specs=[tok] * 4 + [const((1, inner)), const((1, inner)), const(w_out.shape),
                              pl.BlockSpec((1, tm, d), lambda bi, i: (bi, i, 0)),
                              pl.BlockSpec((1, 6, d), lambda bi, i: (mod_row(bi), 0, 0))],
        out_specs=pl.BlockSpec((1, tm, d), lambda bi, i: (bi, i, 0)),
        out_shape=jax.ShapeDtypeStruct((b, t, d), F32),
        compiler_params=_params("parallel", "parallel"),
    )(y0, y1, xbc, z, jnp.repeat(d_skip, HEAD_DIM).reshape(1, inner), norm_w.reshape(1, inner), w_out, x, mod)


K_SH1, K_SC1, K_G1, K_SH2, K_SC2, K_G2 = range(6)


def _conv_ffn_block(x, i, mod, mod_row, norm_ffn, ffn_w_up, ffn_conv_w, ffn_conv_b, ffn_w_down, final_norm):
    hidden = ffn_conv_w.shape[-1]
    w_up = ffn_w_up[i].astype(BF16)
    val, gate = _norm_mod_matmul(x, norm_ffn[i], mod, mod_row, K_SH2, K_SC2,
                                 [w_up[:, :hidden], w_up[:, hidden:]], 256)
    return _ffn_down(val, gate, ffn_conv_w[i], ffn_conv_b[i], ffn_w_down[i].astype(BF16), x, mod, mod_row,
                     K_G2, final_norm, 256)


def kernel(x, c, ctx, c_ctx, ada_w, ada_b, norm_mix, norm_ffn, ev_w_in, ev_w_out, diff_lambda, diff_subln, rwkv_mu, rwkv_w0, rwkv_w_up, rwkv_a0, rwkv_a_up, rwkv_g_up, rwkv_k_k, rwkv_k_a, rwkv_r_k, rwkv_ln_w, rwkv_ln_b, ssm_w_in, ssm_conv_w, ssm_conv_b, ssm_dt_bias, ssm_a_log, ssm_d, ssm_norm_w, ssm_w_out, ffn_w_up, ffn_conv_w, ffn_conv_b, ffn_w_down, final_norm):
    b, t, d = x.shape
    n_ctx = ctx.shape[1]
    depth = ada_w.shape[0]
    assert depth == 2 and b + 1 <= 8 and t % GRID_W == 0

    cvec = jnp.zeros((8, d), F32).at[0:b].set(c).at[b].set(c_ctx)
    mod = _modulation(cvec, ada_w, ada_b).reshape(depth * 8, 6, d)
    ffn_args = (norm_ffn, ffn_w_up, ffn_conv_w, ffn_conv_b, ffn_w_down)

    lat_row = lambda bi: bi
    ctx_row = lambda bi: b
    w_in = ev_w_in[0].astype(BF16)
    w_split = [w_in[:, 0:DIFF_WIDTH], w_in[:, DIFF_WIDTH:2 * DIFF_WIDTH], w_in[:, 2 * DIFF_WIDTH:3 * DIFF_WIDTH],
               w_in[:, 3 * DIFF_WIDTH:]]
    rw = {'mu': rwkv_mu[0], 'w0': rwkv_w0[0], 'w_up': rwkv_w_up[0], 'a0': rwkv_a0[0], 'a_up': rwkv_a_up[0],
          'g_up': rwkv_g_up[0], 'k_k': rwkv_k_k[0], 'k_a': rwkv_k_a[0], 'r_k': rwkv_r_k[0]}
    lam_init = 0.8 - 0.6 * math.exp(-0.3 * 0)
    cos, sin = _rope_tables(t)
    no_rope = (jnp.ones((n_ctx, 128), F32), jnp.zeros((n_ctx, 128), F32))

    q_c, k_c, v_c, pr_c = _norm_mod_matmul(ctx, norm_mix[0], mod, ctx_row, K_SH1, K_SC1, w_split, 512)
    q_l, k_l, v_l, pr_l = _norm_mod_matmul(x, norm_mix[0], mod, lat_row, K_SH1, K_SC1, w_split, 512)
    qz_c, kr_c, vt_c = _qkv_prep(q_c, k_c, v_c, *no_rope, 512)
    qz_l, kr_l, vt_l = _qkv_prep(q_l, k_l, v_l, cos, sin, 512)
    att_c = _diff_attention(qz_c, kr_c, vt_c, diff_lambda[0], diff_subln[0], lam_init, 512, 1280)
    att_l = _diff_attention(qz_l, jnp.concatenate([kr_c, kr_l], axis=1), jnp.concatenate([vt_c, vt_l], axis=3),
                            diff_lambda[0], diff_subln[0], lam_init, 512, 1280)

    r_c, vv_c, kk_c, g_c, bonus_c, ld_c, kd_c, bd_c = _rwkv_prep(pr_c, rw, 256)
    r_l, vv_l, kk_l, g_l, bonus_l, ld_l, kd_l, bd_l = _rwkv_prep(pr_l, rw, 256)
    s_zero = jnp.zeros((b, 2, 2, RWKV_GROUP, RWKV_GROUP), F32)
    y0_c, y1_c, s_ctx = _rwkv_scan(s_zero, r_c, vv_c, kk_c, ld_c, kd_c, bd_c)
    y0_l, y1_l, _ = _rwkv_scan(s_ctx, r_l, vv_l, kk_l, ld_l, kd_l, bd_l)

    w_out = ev_w_out[0].astype(BF16)
    x_ctx = _even_out(att_c, y0_c, y1_c, g_c, bonus_c, rwkv_ln_w[0], rwkv_ln_b[0], w_out, ctx, mod, ctx_row,
                      K_G1, 512)
    x_lat = _even_out(att_l, y0_l, y1_l, g_l, bonus_l, rwkv_ln_w[0], rwkv_ln_b[0], w_out, x, mod, lat_row,
                      K_G1, 512)
    x_ctx = _conv_ffn_block(x_ctx, 0, mod, ctx_row, *ffn_args, None)
    x_lat = _conv_ffn_block(x_lat, 0, mod, lat_row, *ffn_args, None)

    lat_row1 = lambda bi: 8 + bi
    ctx_row1 = lambda bi: 8 + b
    inner = ssm_norm_w.shape[-1]
    conv_ch = ssm_conv_w.shape[-1]
    sw = ssm_w_in[0].astype(BF16)
    sw_dt = jnp.pad(sw[:, inner + conv_ch:], ((0, 0), (0, 128 - 2 * SSM_HEADS)))
    sw_split = [sw[:, :inner], sw[:, inner:inner + conv_ch], sw_dt]
    h_zero = jnp.zeros((b, 2, SSM_HEADS // 2, SSM_STATE, 128), F32)

    _, xbc_c, dt_c = _norm_mod_matmul(x_ctx, norm_mix[1], mod, ctx_row1, K_SH1, K_SC1, sw_split, 256)
    z_l, xbc_l, dt_l = _norm_mod_matmul(x_lat, norm_mix[1], mod, lat_row1, K_SH1, K_SC1, sw_split, 256)
    xbc_c, dtda_c = _ssm_prep(xbc_c, dt_c, ssm_conv_w[0], ssm_conv_b[0], ssm_dt_bias[0], ssm_a_log[0], 256)
    xbc_l, dtda_l = _ssm_prep(xbc_l, dt_l, ssm_conv_w[0], ssm_conv_b[0], ssm_dt_bias[0], ssm_a_log[0], 256)
    _, _, h_ctx = _ssd_scan(h_zero, xbc_c, dtda_c)
    ys0, ys1, _ = _ssd_scan(h_ctx, xbc_l, dtda_l)
    x_lat = _ssm_out(ys0, ys1, xbc_l, z_l, ssm_d[0], ssm_norm_w[0], ssm_w_out[0].astype(BF16), x_lat, mod,
                     lat_row1, K_G1, 256)
    return _conv_ffn_block(x_lat, 1, mod, lat_row1, *ffn_args, final_norm)
```
